```python
import jax, jax.numpy as jnp
from jax import lax
import numpy as np

D_MODEL = 4096
BATCH = 2
SEQ = 8192
DEPTH = 1

SB_HEADS = 16
SB_HEAD_DIM = 128
SB_WIDTH = SB_HEADS * SB_HEAD_DIM
SB_BLOCK = 128
CONV_WIDTH = D_MODEL // 2
CONV_KERNEL = 31
N_BRANCHES = 2
SPLIT_Q = SB_WIDTH
SPLIT_K = 2 * SB_WIDTH
SPLIT_V = 3 * SB_WIDTH
SPLIT_GLU = 3 * SB_WIDTH + 2 * CONV_WIDTH
IN_COLS = SPLIT_GLU + N_BRANCHES * D_MODEL
PEER_HEADS = 8
PEER_KEYS = 128
PEER_EXPERTS = PEER_KEYS * PEER_KEYS
PEER_QDIM = 256
PEER_HALF = PEER_QDIM // 2
PEER_TOPK = 16
PEER_CHUNK = 128
ALPHA = (2 * DEPTH) ** 0.25
BETA = (8 * DEPTH) ** -0.25
LN_EPS = 1e-5
N_MOD = 6

kernel_name = 'hybrid_sbattn_conformer_peer_deepnorm_adaln'


def _layer_norm(x, g=None, b=None):
    xf = x.astype(jnp.float32)
    mu = jnp.mean(xf, axis=-1, keepdims=True)
    var = jnp.mean(jnp.square(xf - mu), axis=-1, keepdims=True)
    y = (xf - mu) * lax.rsqrt(var + LN_EPS)
    if g is not None:
        y = y * g.astype(jnp.float32) + b.astype(jnp.float32)
    return y.astype(x.dtype)


def _split_heads(t):
    b, s, _ = t.shape
    return t.reshape(b, s, SB_HEADS, SB_HEAD_DIM).transpose(0, 2, 1, 3)


def _stick_breaking_attention(q, k, v):
    seq = q.shape[2]
    scale = SB_HEAD_DIM ** -0.5
    outs = []
    for start in range(0, seq, SB_BLOCK):
        end = start + SB_BLOCK
        qb = q[:, :, start:end].astype(jnp.float32)
        kb = k[:, :, :end].astype(jnp.float32)
        z = jnp.einsum('bhqd,bhkd->bhqk', qb, kb) * scale
        t_pos = jnp.arange(start, end)[:, None]
        s_pos = jnp.arange(end)[None, :]
        causal = s_pos < t_pos
        log_beta = jax.nn.log_sigmoid(z)
        log_keep = jnp.where(causal, log_beta - z, 0.0)
        suffix = lax.cumsum(log_keep, axis=3, reverse=True) - log_keep
        a = jnp.where(causal, jnp.exp(log_beta + suffix), 0.0)
        outs.append(jnp.einsum('bhqk,bhkd->bhqd', a.astype(v.dtype), v[:, :, :end]))
    return jnp.concatenate(outs, axis=2)


def _conv_module(u_glu, conv_w, conv_b, ln_g, ln_b):
    a, g = jnp.split(u_glu, 2, axis=-1)
    u = a * jax.nn.sigmoid(g)
    y = lax.conv_general_dilated(
        u, conv_w[:, None, :], window_strides=(1,),
        padding=[(CONV_KERNEL - 1, 0)],
        dimension_numbers=('NWC', 'WIO', 'NWC'),
        feature_group_count=CONV_WIDTH) + conv_b
    return jax.nn.silu(_layer_norm(y, ln_g, ln_b))


def _peer(h, wq, k1, k2, u_tab, v_tab):
    b, s, d = h.shape
    n_tok = b * s
    ht = h.reshape(n_tok, d)
    q = (ht @ wq).reshape(n_tok, PEER_HEADS, PEER_QDIM).astype(jnp.float32)
    q1, q2 = q[..., :PEER_HALF], q[..., PEER_HALF:]
    s1 = jnp.einsum('thd,hnd->thn', q1, k1.astype(jnp.float32))
    s2 = jnp.einsum('thd,hnd->thn', q2, k2.astype(jnp.float32))
    v1, i1 = lax.top_k(s1, PEER_TOPK)
    v2, i2 = lax.top_k(s2, PEER_TOPK)
    cand = (v1[..., :, None] + v2[..., None, :]).reshape(n_tok, PEER_HEADS, PEER_TOPK * PEER_TOPK)
    sc, flat = lax.top_k(cand, PEER_TOPK)
    e1 = jnp.take_along_axis(i1, flat // PEER_TOPK, axis=-1)
    e2 = jnp.take_along_axis(i2, flat % PEER_TOPK, axis=-1)
    idx = e1 * PEER_KEYS + e2
    gate = jax.nn.softmax(sc, axis=-1).astype(h.dtype)
    n_chunks = n_tok // PEER_CHUNK

    def chunk(args):
        hc, ic, gc = args
        ug = jnp.take(u_tab, ic, axis=0)
        act = jax.nn.gelu(jnp.einsum('cd,chkd->chk', hc, ug)) * gc
        vg = jnp.take(v_tab, ic, axis=0)
        return jnp.einsum('chk,chkd->cd', act, vg)

    out = lax.map(chunk, (
        ht.reshape(n_chunks, PEER_CHUNK, d),
        idx.reshape(n_chunks, PEER_CHUNK, PEER_HEADS, PEER_TOPK),
        gate.reshape(n_chunks, PEER_CHUNK, PEER_HEADS, PEER_TOPK)))
    return out.reshape(b, s, d)


def setup_inputs(seed: int = 0) -> dict:
    key = jax.random.key(seed)
    ks = jax.random.split(key, 32)
    f32 = jnp.float32
    L = DEPTH
    D = D_MODEL

    def nrm(k, shape, scale):
        return jax.random.normal(k, shape, f32) * scale

    x = nrm(ks[0], (BATCH, SEQ, D), 1.0)
    c = nrm(ks[1], (BATCH, D), 1.0)
    w_ada = nrm(ks[2], (L, D, N_MOD * D), 0.5 * D ** -0.5)
    b_ada = nrm(ks[3], (L, N_MOD * D), 0.01)
    w_qk = nrm(ks[4], (L, D, 2 * SB_WIDTH), D ** -0.5)
    w_v = nrm(ks[5], (L, D, SB_WIDTH), BETA * D ** -0.5)
    w_glu = nrm(ks[6], (L, D, 2 * CONV_WIDTH), D ** -0.5)
    w_gate = nrm(ks[7], (L, D, N_BRANCHES * D), D ** -0.5)
    w_in = jnp.concatenate([w_qk, w_v, w_glu, w_gate], axis=-1)
    conv_w = nrm(ks[8], (L, CONV_KERNEL, CONV_WIDTH), CONV_KERNEL ** -0.5)
    conv_b = nrm(ks[9], (L, CONV_WIDTH), 0.01)
    conv_ln_g = 1.0 + nrm(ks[10], (L, CONV_WIDTH), 0.05)
    conv_ln_b = nrm(ks[11], (L, CONV_WIDTH), 0.01)
    w_a_proj = nrm(ks[12], (L, SB_WIDTH, D), SB_WIDTH ** -0.5)
    w_b_proj = nrm(ks[13], (L, CONV_WIDTH, D), CONV_WIDTH ** -0.5)
    w_o = nrm(ks[14], (L, D, D), BETA * D ** -0.5)
    ln1_g = 1.0 + nrm(ks[15], (L, D), 0.05)
    ln1_b = nrm(ks[16], (L, D), 0.01)
    peer_wq = nrm(ks[17], (L, D, PEER_HEADS * PEER_QDIM), D ** -0.5)
    peer_k1 = nrm(ks[18], (L, PEER_HEADS, PEER_KEYS, PEER_HALF), PEER_HALF ** -0.5)
    peer_k2 = nrm(ks[19], (L, PEER_HEADS, PEER_KEYS, PEER_HALF), PEER_HALF ** -0.5)
    peer_u = nrm(ks[20], (L, PEER_EXPERTS, D), BETA * D ** -0.5)
    peer_v = nrm(ks[21], (L, PEER_EXPERTS, D), BETA)
    ln2_g = 1.0 + nrm(ks[22], (L, D), 0.05)
    ln2_b = nrm(ks[23], (L, D), 0.01)
    return {'x': x, 'c': c, 'w_ada': w_ada, 'b_ada': b_ada, 'w_in': w_in,
            'conv_w': conv_w, 'conv_b': conv_b, 'conv_ln_g': conv_ln_g, 'conv_ln_b': conv_ln_b,
            'w_a_proj': w_a_proj, 'w_b_proj': w_b_proj, 'w_o': w_o,
            'ln1_g': ln1_g, 'ln1_b': ln1_b,
            'peer_wq': peer_wq, 'peer_k1': peer_k1, 'peer_k2': peer_k2,
            'peer_u': peer_u, 'peer_v': peer_v, 'ln2_g': ln2_g, 'ln2_b': ln2_b}


def reference(x, c, w_ada, b_ada, w_in, conv_w, conv_b, conv_ln_g, conv_ln_b,
              w_a_proj, w_b_proj, w_o, ln1_g, ln1_b,
              peer_wq, peer_k1, peer_k2, peer_u, peer_v, ln2_g, ln2_b):
    b, s, _ = x.shape
    for l in range(DEPTH):
        mod = jnp.einsum('bd,de->be', jax.nn.silu(c), w_ada[l]) + b_ada[l]
        sh_m, sc_m, g_m, sh_f, sc_f, g_f = [m[:, None, :] for m in jnp.split(mod, N_MOD, axis=-1)]

        h = _layer_norm(x) * (1.0 + sc_m) + sh_m
        proj = h @ w_in[l]
        q, k, v, u_glu, gates = jnp.split(proj, [SPLIT_Q, SPLIT_K, SPLIT_V, SPLIT_GLU], axis=-1)
        attn = _stick_breaking_attention(_split_heads(q), _split_heads(k), _split_heads(v))
        y_a = attn.transpose(0, 2, 1, 3).reshape(b, s, SB_WIDTH) @ w_a_proj[l]
        y_b = _conv_module(u_glu, conv_w[l], conv_b[l], conv_ln_g[l], conv_ln_b[l]) @ w_b_proj[l]
        g_a, g_b = jnp.split(jax.nn.sigmoid(gates), N_BRANCHES, axis=-1)
        mixed = (g_a * y_a + g_b * y_b) @ w_o[l]
        x = _layer_norm(ALPHA * x + g_m * mixed, ln1_g[l], ln1_b[l])

        h = _layer_norm(x) * (1.0 + sc_f) + sh_f
        y_f = _peer(h, peer_wq[l], peer_k1[l], peer_k2[l], peer_u[l], peer_v[l])
        x = _layer_norm(ALPHA * x + g_f * y_f, ln2_g[l], ln2_b[l])
    return x
```

```python
import functools

import jax
import jax.numpy as jnp
from jax import lax
from jax.experimental import pallas as pl
from jax.experimental.pallas import tpu as pltpu

F32 = jnp.float32
BF16 = jnp.bfloat16

SB_HEAD_DIM = 128
PEER_TOPK = 16
N_MOD = 6
LN_EPS = 1e-5
CONV_HALO = 32
LANES = 128
SUBLANES = 8
EXP_ZERO_BELOW = -104.0
VMEM_LIMIT_BYTES = 56 * 1024 * 1024


def _params(*semantics):
    return pltpu.CompilerParams(dimension_semantics=semantics, vmem_limit_bytes=VMEM_LIMIT_BYTES)


def _tile(dim, pref):
    t = min(dim, pref)
    assert dim % t == 0, (dim, pref)
    return t


def _layer_norm_rows(x):
    mu = jnp.mean(x, axis=-1, keepdims=True)
    xc = x - mu
    var = jnp.mean(xc * xc, axis=-1, keepdims=True)
    return xc * lax.rsqrt(var + LN_EPS)


def _ada_kernel(c_ref, w_ref, b_ref, o_ref):
    c = c_ref[...]
    s = c * jax.nn.sigmoid(c)
    o_ref[...] = jnp.dot(s.astype(BF16), w_ref[...].astype(BF16),
                         preferred_element_type=F32) + b_ref[...]


def _ada_mod(c, w_ada, b_ada):
    b, d = c.shape
    n = w_ada.shape[1]
    rows = 8
    c_pad = jnp.zeros((rows, d), F32).at[:b].set(c)
    tn = _tile(n, 512)
    out = pl.pallas_call(
        _ada_kernel,
        out_shape=jax.ShapeDtypeStruct((rows, n), F32),
        grid=(n // tn,),
        in_specs=[pl.BlockSpec((rows, d), lambda j: (0, 0)),
                  pl.BlockSpec((d, tn), lambda j: (0, j)),
                  pl.BlockSpec((1, tn), lambda j: (0, j))],
        out_specs=pl.BlockSpec((rows, tn), lambda j: (0, j)),
        compiler_params=_params("arbitrary"),
        name="ada_mod",
    )(c_pad, w_ada, b_ada.reshape(1, n))
    return out[:b].reshape(b * N_MOD, 1, d)


def _ln_mod_kernel(x_ref, sc_ref, sh_ref, o_ref):
    y = _layer_norm_rows(x_ref[...])
    o_ref[...] = (y * (1.0 + sc_ref[0]) + sh_ref[0]).astype(o_ref.dtype)


def _ln_mod(x2d, mod3, seq, k_scale, k_shift):
    t, d = x2d.shape
    tm = _tile(seq, 256)
    per_b = seq // tm
    return pl.pallas_call(
        _ln_mod_kernel,
        out_shape=jax.ShapeDtypeStruct((t, d), BF16),
        grid=(t // tm,),
        in_specs=[pl.BlockSpec((tm, d), lambda i: (i, 0)),
                  pl.BlockSpec((1, 1, d), lambda i: ((i // per_b) * N_MOD + k_scale, 0, 0)),
                  pl.BlockSpec((1, 1, d), lambda i: ((i // per_b) * N_MOD + k_shift, 0, 0))],
        out_specs=pl.BlockSpec((tm, d), lambda i: (i, 0)),
        compiler_params=_params("parallel"),
        name="ln_mod",
    )(x2d, mod3, mod3)


def _matmul_kernel(a_ref, w_ref, o_ref):
    o_ref[...] = jnp.dot(a_ref[...], w_ref[...], preferred_element_type=F32).astype(o_ref.dtype)


def _matmul(a, w, out_dtype, tm_pref, tn_pref, name):
    m, k = a.shape
    n = w.shape[1]
    tm, tn = _tile(m, tm_pref), _tile(n, tn_pref)
    return pl.pallas_call(
        _matmul_kernel,
        out_shape=jax.ShapeDtypeStruct((m, n), out_dtype),
        grid=(m // tm, n // tn),
        in_specs=[pl.BlockSpec((tm, k), lambda i, j: (i, 0)),
                  pl.BlockSpec((k, tn), lambda i, j: (0, j))],
        out_specs=pl.BlockSpec((tm, tn), lambda i, j: (i, j)),
        compiler_params=_params("parallel", "arbitrary"),
        name=name,
    )(a, w)


def _sb_attn_kernel(q_ref, k_ref, v_ref, o_ref, *, tq, tk, scale):
    qi = pl.program_id(2)
    q = q_ref[...]
    row = qi * tq + lax.broadcasted_iota(jnp.int32, (tq, 1), 0)
    tri = (lax.broadcasted_iota(jnp.int32, (tk, tk), 0)
           > lax.broadcasted_iota(jnp.int32, (tk, tk), 1)).astype(BF16)
    hd = q.shape[1]

    def cond(state):
        kb, carry, _ = state
        return jnp.logical_and(kb >= 0, jnp.max(carry) > EXP_ZERO_BELOW)

    def body(state):
        kb, carry, acc = state
        start = pl.multiple_of(kb * tk, tk)
        k = k_ref[pl.ds(start, tk), :]
        v = v_ref[pl.ds(start, tk), :]
        z = lax.dot_general(q, k, (((1,), (1,)), ((), ())), preferred_element_type=F32) * scale
        col = start + lax.broadcasted_iota(jnp.int32, (1, tk), 1)
        causal = col < row
        log_beta = jnp.minimum(z, 0.0) - jnp.log1p(jnp.exp(-jnp.abs(z)))
        log_keep = jnp.where(causal, log_beta - z, 0.0)
        hi = log_keep.astype(BF16)
        lo = (log_keep - hi.astype(F32)).astype(BF16)
        later = (jnp.dot(hi, tri, preferred_element_type=F32)
                 + jnp.dot(lo, tri, preferred_element_type=F32))
        a = jnp.where(causal, jnp.exp(log_beta + later + carry), 0.0)
        acc = acc + jnp.dot(a.astype(BF16), v, preferred_element_type=F32)
        carry = carry + jnp.sum(log_keep, axis=1, keepdims=True)
        return kb - 1, carry, acc

    kb0 = (qi * tq + tq - 1) // tk
    init = (kb0, jnp.zeros((tq, 1), F32), jnp.zeros((tq, hd), F32))
    _, _, acc = lax.while_loop(cond, body, init)
    o_ref[...] = acc.astype(o_ref.dtype)


def _sb_attention(proj, batch, seq, n_heads):
    t = proj.shape[0]
    hd = SB_HEAD_DIM
    tq = _tile(seq, 256)
    tk = tq
    per_b = seq // tq
    kern = functools.partial(_sb_attn_kernel, tq=tq, tk=tk, scale=hd ** -0.5)
    return pl.pallas_call(
        kern,
        out_shape=jax.ShapeDtypeStruct((t, n_heads * hd), BF16),
        grid=(batch, n_heads, per_b),
        in_specs=[pl.BlockSpec((tq, hd), lambda b, h, i: (b * per_b + i, h)),
                  pl.BlockSpec((seq, hd), lambda b, h, i: (b, n_heads + h)),
                  pl.BlockSpec((seq, hd), lambda b, h, i: (b, 2 * n_heads + h))],
        out_specs=pl.BlockSpec((tq, hd), lambda b, h, i: (b * per_b + i, h)),
        compiler_params=_params("parallel", "parallel", "arbitrary"),
        name="sb_attention",
    )(proj, proj, proj)


def _conv_kernel(a_ref, g_ref, ah_ref, gh_ref, w_ref, cb_ref, lg_ref, lb_ref, o_ref, ext_ref,
                 *, ts, taps):
    si = pl.program_id(1)
    u = a_ref[...].astype(F32) * jax.nn.sigmoid(g_ref[...].astype(F32))
    uh = ah_ref[...].astype(F32) * jax.nn.sigmoid(gh_ref[...].astype(F32))
    uh = jnp.where(si > 0, uh, 0.0)
    ext_ref[0:CONV_HALO, :] = uh
    ext_ref[CONV_HALO:CONV_HALO + ts, :] = u
    off = CONV_HALO - (taps - 1)
    acc = jnp.zeros(u.shape, F32) + cb_ref[...]
    for k in range(taps):
        acc = acc + w_ref[k:k + 1, :] * ext_ref[off + k:off + k + ts, :]
    y = _layer_norm_rows(acc) * lg_ref[...] + lb_ref[...]
    o_ref[...] = (y * jax.nn.sigmoid(y)).astype(o_ref.dtype)


def _conv_module(proj, conv_w, conv_b, ln_g, ln_b, batch, seq, col0):
    t = proj.shape[0]
    taps, c = conv_w.shape
    assert taps - 1 <= CONV_HALO and col0 % c == 0
    ts = _tile(seq, 256)
    assert ts % CONV_HALO == 0
    per_b = seq // ts
    cb = col0 // c
    halo_per_tile = ts // CONV_HALO

    def halo_map(col):
        return lambda b, i: (jnp.maximum((b * per_b + i) * halo_per_tile - 1, 0), col)

    kern = functools.partial(_conv_kernel, ts=ts, taps=taps)
    row = lambda b, i: (0, 0)
    return pl.pallas_call(
        kern,
        out_shape=jax.ShapeDtypeStruct((t, c), BF16),
        grid=(batch, per_b),
        in_specs=[pl.BlockSpec((ts, c), lambda b, i: (b * per_b + i, cb)),
                  pl.BlockSpec((ts, c), lambda b, i: (b * per_b + i, cb + 1)),
                  pl.BlockSpec((CONV_HALO, c), halo_map(cb)),
                  pl.BlockSpec((CONV_HALO, c), halo_map(cb + 1)),
                  pl.BlockSpec((taps, c), row),
                  pl.BlockSpec((1, c), row),
                  pl.BlockSpec((1, c), row),
                  pl.BlockSpec((1, c), row)],
        out_specs=pl.BlockSpec((ts, c), lambda b, i: (b * per_b + i, 0)),
        scratch_shapes=[pltpu.VMEM((CONV_HALO + ts, c), F32)],
        compiler_params=_params("parallel", "arbitrary"),
        name="conv_module",
    )(proj, proj, proj, proj, conv_w, conv_b.reshape(1, c), ln_g.reshape(1, c), ln_b.reshape(1, c))


def _merge_kernel(a1_ref, a2_ref, w1_ref, w2_ref, ga_ref, gb_ref, o_ref):
    ya = jnp.dot(a1_ref[...], w1_ref[...], preferred_element_type=F32)
    yb = jnp.dot(a2_ref[...], w2_ref[...], preferred_element_type=F32)
    ga = jax.nn.sigmoid(ga_ref[...].astype(F32))
    gb = jax.nn.sigmoid(gb_ref[...].astype(F32))
    o_ref[...] = (ga * ya + gb * yb).astype(o_ref.dtype)


def _merge(attn, cv, w_a, w_b, proj, gate_col0):
    t, ka = attn.shape
    kb = cv.shape[1]
    d = w_a.shape[1]
    tm, tn = _tile(t, 512), _tile(d, 1024)
    assert gate_col0 % tn == 0
    ga0 = gate_col0 // tn
    gb0 = (gate_col0 + d) // tn
    return pl.pallas_call(
        _merge_kernel,
        out_shape=jax.ShapeDtypeStruct((t, d), BF16),
        grid=(t // tm, d // tn),
        in_specs=[pl.BlockSpec((tm, ka), lambda i, j: (i, 0)),
                  pl.BlockSpec((tm, kb), lambda i, j: (i, 0)),
                  pl.BlockSpec((ka, tn), lambda i, j: (0, j)),
                  pl.BlockSpec((kb, tn), lambda i, j: (0, j)),
                  pl.BlockSpec((tm, tn), lambda i, j: (i, ga0 + j)),
                  pl.BlockSpec((tm, tn), lambda i, j: (i, gb0 + j))],
        out_specs=pl.BlockSpec((tm, tn), lambda i, j: (i, j)),
        compiler_params=_params("parallel", "arbitrary"),
        name="branch_merge",
    )(attn, cv, w_a, w_b, proj, proj)


def _oproj_kernel(a_ref, w_ref, x_ref, g_ref, o_ref, *, alpha):
    y = jnp.dot(a_ref[...], w_ref[...], preferred_element_type=F32)
    o_ref[...] = alpha * x_ref[...] + g_ref[0] * y


def _oproj_residual(a, w, x2d, mod3, seq, k_gate, alpha):
    t, k = a.shape
    d = w.shape[1]
    tm, tn = _tile(seq, 1024), _tile(d, 512)
    per_b = seq // tm
    return pl.pallas_call(
        functools.partial(_oproj_kernel, alpha=alpha),
        out_shape=jax.ShapeDtypeStruct((t, d), F32),
        grid=(t // tm, d // tn),
        in_specs=[pl.BlockSpec((tm, k), lambda i, j: (i, 0)),
                  pl.BlockSpec((k, tn), lambda i, j: (0, j)),
                  pl.BlockSpec((tm, tn), lambda i, j: (i, j)),
                  pl.BlockSpec((1, 1, tn), lambda i, j: ((i // per_b) * N_MOD + k_gate, 0, j))],
        out_specs=pl.BlockSpec((tm, tn), lambda i, j: (i, j)),
        compiler_params=_params("parallel", "arbitrary"),
        name="out_proj_residual",
    )(a, w, x2d, mod3)


def _ln1_kernel(r_ref, g_ref, b_ref, sc_ref, sh_ref, x1_ref, h_ref):
    x1 = _layer_norm_rows(r_ref[...]) * g_ref[...] + b_ref[...]
    x1_ref[...] = x1
    h_ref[...] = (_layer_norm_rows(x1) * (1.0 + sc_ref[0]) + sh_ref[0]).astype(h_ref.dtype)


def _ln1_and_mod(r, ln_g, ln_b, mod3, seq, k_scale, k_shift):
    t, d = r.shape
    tm = _tile(seq, 256)
    per_b = seq // tm
    row = lambda i: (0, 0)
    return pl.pallas_call(
        _ln1_kernel,
        out_shape=(jax.ShapeDtypeStruct((t, d), F32), jax.ShapeDtypeStruct((t, d), BF16)),
        grid=(t // tm,),
        in_specs=[pl.BlockSpec((tm, d), lambda i: (i, 0)),
                  pl.BlockSpec((1, d), row),
                  pl.BlockSpec((1, d), row),
                  pl.BlockSpec((1, 1, d), lambda i: ((i // per_b) * N_MOD + k_scale, 0, 0)),
                  pl.BlockSpec((1, 1, d), lambda i: ((i // per_b) * N_MOD + k_shift, 0, 0))],
        out_specs=(pl.BlockSpec((tm, d), lambda i: (i, 0)),
                   pl.BlockSpec((tm, d), lambda i: (i, 0))),
        compiler_params=_params("parallel"),
        name="ln1_and_mod",
    )(r, ln_g.reshape(1, d), ln_b.reshape(1, d), mod3, mod3)


def _top16_rows(s):
    n = s.shape[0]
    iota = lax.broadcasted_iota(jnp.int32, s.shape, 0).astype(F32)
    rank = jnp.full(s.shape, float(PEER_TOPK), F32)
    vals = []
    for r in range(PEER_TOPK):
        m = jnp.max(s, axis=0, keepdims=True)
        first = jnp.min(jnp.where(s == m, iota, float(n)), axis=0, keepdims=True)
        sel = iota == first
        rank = jnp.where(sel, float(r), rank)
        s = jnp.where(sel, -jnp.inf, s)
        vals.append(m)
    return jnp.concatenate(vals, axis=0), rank


def _top16_pairs(v1, v2):
    k = v1.shape[0]
    iota = lax.broadcasted_iota(jnp.int32, v1.shape, 0).astype(F32)
    count = jnp.zeros(v1.shape, F32)
    front = v1 + v2[0:1]
    top = None
    denom = jnp.zeros((1, v1.shape[1]), F32)
    for _ in range(PEER_TOPK):
        m = jnp.max(front, axis=0, keepdims=True)
        if top is None:
            top = m
        first = jnp.min(jnp.where(front == m, iota, float(k)), axis=0, keepdims=True)
        pick = iota == first
        denom = denom + jnp.exp(m - top)
        count = jnp.where(pick, count + 1.0, count)
        new_count = jnp.max(jnp.where(pick, count, 0.0), axis=0, keepdims=True)
        v2_next = jnp.max(jnp.where(iota == new_count, v2, -jnp.inf), axis=0, keepdims=True)
        v1_row = jnp.max(jnp.where(pick, v1, -jnp.inf), axis=0, keepdims=True)
        front = jnp.where(pick, v1_row + v2_next, front)
    return count, denom


def _route_kernel(q_ref, k1_ref, k2_ref, cnt_ref, w1_ref, r2_ref, p2_ref, *, n_heads, half):
    contract_last = (((1,), (1,)), ((), ()))
    for h in range(n_heads):
        q1 = q_ref[:, (2 * h) * half:(2 * h + 1) * half]
        q2 = q_ref[:, (2 * h + 1) * half:(2 * h + 2) * half]
        s1 = lax.dot_general(k1_ref[h], q1, contract_last, preferred_element_type=F32)
        s2 = lax.dot_general(k2_ref[h], q2, contract_last, preferred_element_type=F32)
        v1, rank1 = _top16_rows(s1)
        v2, rank2 = _top16_rows(s2)
        count, denom = _top16_pairs(v1, v2)
        cnt = jnp.zeros(s1.shape, F32)
        for i in range(PEER_TOPK):
            cnt = jnp.where(rank1 == float(i), count[i:i + 1], cnt)
        cnt_ref[h] = cnt
        w1_ref[h] = jnp.exp(s1 - v1[0:1]) / denom
        r2_ref[h] = rank2
        p2_ref[h] = jnp.exp(s2 - v2[0:1])


def _peer_route(q, k1, k2):
    t = q.shape[0]
    n_heads, n_keys, half = k1.shape
    tt = _tile(t, LANES)
    shape = jax.ShapeDtypeStruct((n_heads, n_keys, t), F32)
    spec = pl.BlockSpec((n_heads, n_keys, tt), lambda i: (0, 0, i))
    kspec = pl.BlockSpec((n_heads, n_keys, half), lambda i: (0, 0, 0))
    return pl.pallas_call(
        functools.partial(_route_kernel, n_heads=n_heads, half=half),
        out_shape=(shape, shape, shape, shape),
        grid=(t // tt,),
        in_specs=[pl.BlockSpec((tt, q.shape[1]), lambda i: (i, 0)), kspec, kspec],
        out_specs=(spec, spec, spec, spec),
        compiler_params=_params("parallel"),
        name="peer_route",
    )(q, k1, k2)


def _peer_kernel(h_ref, u_ref, v_ref, cnt_ref, w1_ref, r2_ref, p2_ref, o_ref, act_ref,
                 *, n_heads, n_keys, te, tt):
    ei = pl.program_id(1)

    @pl.when(ei == 0)
    def _():
        o_ref[...] = jnp.zeros_like(o_ref)

    s = lax.dot_general(u_ref[...], h_ref[...], (((1,), (1,)), ((), ())),
                        preferred_element_type=F32)
    n_e1 = te // n_keys
    groups = SUBLANES // n_e1
    base = pl.multiple_of((ei // groups) * SUBLANES, SUBLANES)
    sub = ei % groups

    def key1_row(ref, h, e, lanes):
        grp = ref[h, pl.ds(base, SUBLANES), lanes]
        row = grp[e:e + 1]
        for s in range(1, groups):
            row = jnp.where(sub == s, grp[s * n_e1 + e:s * n_e1 + e + 1], row)
        return row

    for e in range(n_e1):
        for c in range(tt // LANES):
            lanes = slice(c * LANES, (c + 1) * LANES)
            gate = jnp.zeros((n_keys, LANES), F32)
            for h in range(n_heads):
                cnt = key1_row(cnt_ref, h, e, lanes)
                w1 = key1_row(w1_ref, h, e, lanes)
                gate = gate + jnp.where(r2_ref[h, :, lanes] < cnt, p2_ref[h, :, lanes], 0.0) * w1
            act = jax.nn.gelu(s[e * n_keys:(e + 1) * n_keys, lanes]) * gate
            act_ref[e * n_keys:(e + 1) * n_keys, lanes] = act.astype(act_ref.dtype)
    o_ref[...] += lax.dot_general(act_ref[...], v_ref[...], (((0,), (0,)), ((), ())),
                                  preferred_element_type=F32)


def _peer_experts(h, u_tab, v_tab, cnt, w1, r2, p2):
    t, d = h.shape
    n_heads, n_keys, _ = cnt.shape
    n_exp = u_tab.shape[0]
    tt = _tile(t, 512)
    te = _tile(n_exp, 512)
    assert te % n_keys == 0 and SUBLANES % (te // n_keys) == 0 and tt % LANES == 0
    once = pl.Buffered(1)
    mspec = pl.BlockSpec((n_heads, n_keys, tt), lambda i, e: (0, 0, i), pipeline_mode=once)
    kern = functools.partial(_peer_kernel, n_heads=n_heads, n_keys=n_keys, te=te, tt=tt)
    return pl.pallas_call(
        kern,
        out_shape=jax.ShapeDtypeStruct((t, d), F32),
        grid=(t // tt, n_exp // te),
        in_specs=[pl.BlockSpec((tt, d), lambda i, e: (i, 0), pipeline_mode=once),
                  pl.BlockSpec((te, d), lambda i, e: (e, 0)),
                  pl.BlockSpec((te, d), lambda i, e: (e, 0)),
                  mspec, mspec, mspec, mspec],
        out_specs=pl.BlockSpec((tt, d), lambda i, e: (i, 0)),
        scratch_shapes=[pltpu.VMEM((te, tt), BF16)],
        compiler_params=_params("parallel", "arbitrary"),
        name="peer_experts",
    )(h, u_tab, v_tab, cnt, w1, r2, p2)


def _ln2_kernel(x_ref, y_ref, gate_ref, g_ref, b_ref, o_ref, *, alpha):
    r = alpha * x_ref[...] + gate_ref[0] * y_ref[...]
    o_ref[...] = _layer_norm_rows(r) * g_ref[...] + b_ref[...]


def _residual_ln2(x1, y, mod3, ln_g, ln_b, seq, k_gate, alpha):
    t, d = x1.shape
    tm = _tile(seq, 256)
    per_b = seq // tm
    row = lambda i: (0, 0)
    return pl.pallas_call(
        functools.partial(_ln2_kernel, alpha=alpha),
        out_shape=jax.ShapeDtypeStruct((t, d), F32),
        grid=(t // tm,),
        in_specs=[pl.BlockSpec((tm, d), lambda i: (i, 0)),
                  pl.BlockSpec((tm, d), lambda i: (i, 0)),
                  pl.BlockSpec((1, 1, d), lambda i: ((i // per_b) * N_MOD + k_gate, 0, 0)),
                  pl.BlockSpec((1, d), row),
                  pl.BlockSpec((1, d), row)],
        out_specs=pl.BlockSpec((tm, d), lambda i: (i, 0)),
        compiler_params=_params("parallel"),
        name="residual_ln2",
    )(x1, y, mod3, ln_g.reshape(1, d), ln_b.reshape(1, d))


def kernel(x, c, w_ada, b_ada, w_in, conv_w, conv_b, conv_ln_g, conv_ln_b, w_a_proj, w_b_proj, w_o, ln1_g, ln1_b, peer_wq, peer_k1, peer_k2, peer_u, peer_v, ln2_g, ln2_b):
    batch, seq, d = x.shape
    depth = w_ada.shape[0]
    alpha = (2 * depth) ** 0.25
    sb_width = w_a_proj.shape[1]
    n_sb_heads = sb_width // SB_HEAD_DIM
    conv_width = conv_w.shape[2]
    glu_col0 = 3 * sb_width
    gate_col0 = glu_col0 + 2 * conv_width
    n_peer_heads = peer_k1.shape[1]

    x2d = x.reshape(batch * seq, d)
    for l in range(depth):
        mod3 = _ada_mod(c, w_ada[l], b_ada[l])

        h = _ln_mod(x2d, mod3, seq, k_scale=1, k_shift=0)
        proj = _matmul(h, w_in[l].astype(BF16), BF16, 1024, 1024, "in_proj")
        attn = _sb_attention(proj, batch, seq, n_sb_heads)
        cv = _conv_module(proj, conv_w[l], conv_b[l], conv_ln_g[l], conv_ln_b[l],
                          batch, seq, glu_col0)
        mixed = _merge(attn, cv, w_a_proj[l].astype(BF16), w_b_proj[l].astype(BF16),
                       proj, gate_col0)
        r = _oproj_residual(mixed, w_o[l].astype(BF16), x2d, mod3, seq, k_gate=2, alpha=alpha)
        x1, h2 = _ln1_and_mod(r, ln1_g[l], ln1_b[l], mod3, seq, k_scale=4, k_shift=3)

        q = _matmul(h2, peer_wq[l].astype(BF16), BF16, 1024, 1024, "peer_query")
        cnt, w1, r2, p2 = _peer_route(q, peer_k1[l].astype(BF16), peer_k2[l].astype(BF16))
        y = _peer_experts(h2, peer_u[l].astype(BF16), peer_v[l].astype(BF16), cnt, w1, r2, p2)
        x2d = _residual_ln2(x1, y, mod3, ln2_g[l], ln2_b[l], seq, k_gate=5, alpha=alpha)
    return x2d.reshape(batch, seq, d)
```
